```python
import math
import jax, jax.numpy as jnp
from jax import lax
import numpy as np

D_MODEL = 1024
BATCH = 4
SEQ = 8192
DEPTH = 1

HEAD_DIM = D_MODEL // 16
N_ATTN_HEADS = 12
N_GMLP_GROUPS = 4
GMLP_DIM = HEAD_DIM
ATTN_WIDTH = N_ATTN_HEADS * HEAD_DIM
GMLP_WIDTH = N_GMLP_GROUPS * GMLP_DIM
MIX_WIDTH = ATTN_WIDTH + GMLP_WIDTH
IN_WIDTH = 3 * ATTN_WIDTH + 2 * GMLP_WIDTH
CHUNK = 128
BLOCK = 128
DILATED_PATTERNS = ((128, 1), (512, 4), (2048, 16))
D_FF = 4 * D_MODEL
EPS = 1e-6

kernel_name = "hymba_gmlp_longnet_alibi_block"


def alibi_slopes(n):
    def pow2_slopes(m):
        start = 2.0 ** (-8.0 / m)
        return [start ** (i + 1) for i in range(m)]
    if math.log2(n).is_integer():
        s = pow2_slopes(n)
    else:
        c = 2 ** int(math.floor(math.log2(n)))
        s = pow2_slopes(c) + pow2_slopes(2 * c)[0::2][: n - c]
    return np.asarray(s, dtype=np.float32)


def rms_norm(x, g):
    xf = x.astype(jnp.float32)
    y = xf * lax.rsqrt(jnp.mean(xf * xf, axis=-1, keepdims=True) + EPS)
    return (y * g.astype(jnp.float32)).astype(x.dtype)


def layer_norm(x, g, b):
    xf = x.astype(jnp.float32)
    mu = jnp.mean(xf, axis=-1, keepdims=True)
    var = jnp.mean(jnp.square(xf - mu), axis=-1, keepdims=True)
    y = (xf - mu) * lax.rsqrt(var + EPS)
    return (y * g.astype(jnp.float32) + b.astype(jnp.float32)).astype(x.dtype)


def chunked_spatial_gating(u, z, ln_g, ln_b, w_s, b_s):
    B, S, G, C = u.shape
    u = jax.nn.gelu(u)
    z = layer_norm(jax.nn.gelu(z), ln_g, ln_b)
    zc = z.reshape(B, S // CHUNK, CHUNK, G, C)
    causal = jnp.tril(jnp.ones((CHUNK, CHUNK), dtype=w_s.dtype))
    ws = w_s * causal[None]
    mixed = jnp.einsum('gts,bnsgc->bntgc', ws, zc) + b_s.T[None, None, :, :, None]
    return u * mixed.reshape(B, S, G, C)


def dilated_window_attention(q, k, v, slopes, window, dilation):
    B, S, H, Dh = q.shape
    span = BLOCK * dilation
    S_pad = -(-S // span) * span
    pad = S_pad - S
    L = S_pad // dilation
    nb = L // BLOCK

    def to_sub(t):
        t = jnp.pad(t.astype(jnp.float32), ((0, 0), (0, pad), (0, 0), (0, 0)))
        t = t.reshape(B, L, dilation, H, Dh).transpose(0, 2, 3, 1, 4)
        return t.reshape(B, dilation, H, nb, BLOCK, Dh)

    qs, ks, vs = to_sub(q), to_sub(k), to_sub(v)
    blk_pad = ((0, 0), (0, 0), (0, 0), (1, 0), (0, 0), (0, 0))
    kb = jnp.concatenate([jnp.pad(ks, blk_pad)[:, :, :, :-1], ks], axis=4)
    vb = jnp.concatenate([jnp.pad(vs, blk_pad)[:, :, :, :-1], vs], axis=4)

    scores = jnp.einsum('brhnqd,brhnkd->brhnqk', qs, kb)
    qi = jnp.arange(BLOCK)[:, None]
    kj = jnp.arange(2 * BLOCK)[None, :]
    steps = qi + BLOCK - kj
    band = (steps >= 0) & (steps <= window // dilation)
    blk = jnp.arange(nb)[:, None, None]
    valid = band[None] & ~((blk == 0) & (kj[None] < BLOCK))
    alibi = -slopes[:, None, None] * (steps * dilation).astype(jnp.float32)[None]
    scores = scores + alibi[None, None, :, None]
    scores = jnp.where(valid[None, None, None], scores, -jnp.inf)

    m = jnp.max(scores, axis=-1, keepdims=True)
    p = jnp.exp(scores - m)
    l = jnp.sum(p, axis=-1, keepdims=True)
    o = jnp.einsum('brhnqk,brhnkd->brhnqd', p, vb) / l
    lse = (m + jnp.log(l))[..., 0]

    o = o.reshape(B, dilation, H, L, Dh).transpose(0, 3, 1, 2, 4).reshape(B, S_pad, H, Dh)[:, :S]
    lse = lse.reshape(B, dilation, H, L).transpose(0, 3, 1, 2).reshape(B, S_pad, H)[:, :S]
    return o, lse


def mixture_of_dilations(q, k, v, slopes):
    outs, lses = [], []
    for window, dilation in DILATED_PATTERNS:
        o, lse = dilated_window_attention(q, k, v, slopes, window, dilation)
        outs.append(o)
        lses.append(lse)
    w = jax.nn.softmax(jnp.stack(lses, axis=0), axis=0)
    return jnp.sum(w[..., None] * jnp.stack(outs, axis=0), axis=0)


def setup_inputs(seed: int = 0) -> dict:
    key = jax.random.key(seed)
    ks = jax.random.split(key, 16)
    f32 = jnp.float32
    nrm = lambda k, shape, scale: jax.random.normal(k, shape, f32) * scale
    G, C = N_GMLP_GROUPS, GMLP_DIM
    return {
        "x": nrm(ks[0], (BATCH, SEQ, D_MODEL), 1.0),
        "norm1_g": 1.0 + nrm(ks[1], (DEPTH, D_MODEL), 0.02),
        "w_in": nrm(ks[2], (DEPTH, D_MODEL, IN_WIDTH), D_MODEL ** -0.5),
        "sgu_ln_g": 1.0 + nrm(ks[3], (DEPTH, G, C), 0.02),
        "sgu_ln_b": nrm(ks[4], (DEPTH, G, C), 0.02),
        "sgu_w": nrm(ks[5], (DEPTH, G, CHUNK, CHUNK), CHUNK ** -0.5),
        "sgu_b": 1.0 + nrm(ks[6], (DEPTH, G, CHUNK), 0.02),
        "attn_out_g": 1.0 + nrm(ks[7], (DEPTH, ATTN_WIDTH), 0.02),
        "gmlp_out_g": 1.0 + nrm(ks[8], (DEPTH, GMLP_WIDTH), 0.02),
        "w_out": nrm(ks[9], (DEPTH, MIX_WIDTH, D_MODEL), MIX_WIDTH ** -0.5),
        "norm2_g": 1.0 + nrm(ks[10], (DEPTH, D_MODEL), 0.02),
        "w_ff1": nrm(ks[11], (DEPTH, D_MODEL, D_FF), D_MODEL ** -0.5),
        "w_ff2": nrm(ks[12], (DEPTH, D_FF, D_MODEL), D_FF ** -0.5),
        "final_norm_g": 1.0 + nrm(ks[13], (D_MODEL,), 0.02),
    }


def reference(x, norm1_g, w_in, sgu_ln_g, sgu_ln_b, sgu_w, sgu_b, attn_out_g, gmlp_out_g,
              w_out, norm2_g, w_ff1, w_ff2, final_norm_g):
    B, S, _ = x.shape
    slopes = jnp.asarray(alibi_slopes(N_ATTN_HEADS), dtype=jnp.float32)
    scale = HEAD_DIM ** -0.5
    A, Gw = ATTN_WIDTH, GMLP_WIDTH
    h = x
    for l in range(DEPTH):
        hn = rms_norm(h, norm1_g[l])
        proj = hn @ w_in[l]
        q = (proj[..., :A] * scale).reshape(B, S, N_ATTN_HEADS, HEAD_DIM)
        k = proj[..., A:2 * A].reshape(B, S, N_ATTN_HEADS, HEAD_DIM)
        v = proj[..., 2 * A:3 * A].reshape(B, S, N_ATTN_HEADS, HEAD_DIM)
        u = proj[..., 3 * A:3 * A + Gw].reshape(B, S, N_GMLP_GROUPS, GMLP_DIM)
        z = proj[..., 3 * A + Gw:].reshape(B, S, N_GMLP_GROUPS, GMLP_DIM)

        attn = mixture_of_dilations(q, k, v, slopes).astype(h.dtype).reshape(B, S, A)
        gmlp = chunked_spatial_gating(u, z, sgu_ln_g[l], sgu_ln_b[l], sgu_w[l], sgu_b[l]).reshape(B, S, Gw)

        mixed = jnp.concatenate([rms_norm(attn, attn_out_g[l]), rms_norm(gmlp, gmlp_out_g[l])], axis=-1)
        h = h + mixed @ w_out[l]

        hn = rms_norm(h, norm2_g[l])
        h = h + jnp.square(jax.nn.relu(hn @ w_ff1[l])) @ w_ff2[l]
    return rms_norm(h, final_norm_g)
```

```python
import functools
import math

import jax
import jax.numpy as jnp
import numpy as np
from jax import lax
from jax.experimental import pallas as pl
from jax.experimental.pallas import tpu as pltpu

D_MODEL = 1024
HEAD_DIM = 64
N_HEADS = 12
N_GROUPS = 4
GROUP_DIM = 64
ATTN_W = N_HEADS * HEAD_DIM
GMLP_W = N_GROUPS * GROUP_DIM
IN_W = 3 * ATTN_W + 2 * GMLP_W
CHUNK = 128
PATTERNS = ((128, 1), (512, 4), (2048, 16))
D_FF = 4 * D_MODEL
EPS = 1e-6

LANES = 128
VMEM_LIMIT_BYTES = 60000 * 1024

ROW_TILE = 512
Q_TILE = 512
FF_TILE = 1024
N_PAIRS = N_HEADS // 2


def _alibi_slopes(n):
    def pow2_slopes(m):
        start = 2.0 ** (-8.0 / m)
        return [start ** (i + 1) for i in range(m)]
    if math.log2(n).is_integer():
        s = pow2_slopes(n)
    else:
        c = 2 ** int(math.floor(math.log2(n)))
        s = pow2_slopes(c) + pow2_slopes(2 * c)[0::2][: n - c]
    return np.asarray(s, dtype=np.float32)


def _rms_norm(x, g):
    ms = jnp.mean(x * x, axis=-1, keepdims=True)
    return x * lax.rsqrt(ms + EPS) * g


def _split_dot(x, m):
    hi = x.astype(jnp.bfloat16)
    lo = (x - hi.astype(jnp.float32)).astype(jnp.bfloat16)
    return (jnp.dot(hi, m, preferred_element_type=jnp.float32)
            + jnp.dot(lo, m, preferred_element_type=jnp.float32))


def _in_proj_kernel(x_ref, g1_ref, w_ref, lng_ref, lnb_ref, ws_ref, bs_ref, gg_ref,
                    q_ref, k_ref, v_ref, gm_ref):
    x = x_ref[...]
    hn = _rms_norm(x, g1_ref[...]).astype(jnp.bfloat16)

    def proj(lo, width):
        return jnp.dot(hn, w_ref[:, lo:lo + width], preferred_element_type=jnp.float32)

    q_ref[...] = (proj(0, ATTN_W) * (HEAD_DIM ** -0.5)).astype(jnp.bfloat16)
    k_ref[...] = proj(ATTN_W, ATTN_W).astype(jnp.bfloat16)
    v_ref[...] = proj(2 * ATTN_W, ATTN_W).astype(jnp.bfloat16)
    u = proj(3 * ATTN_W, GMLP_W)
    z = proj(3 * ATTN_W + GMLP_W, GMLP_W)

    row_g = lax.broadcasted_iota(jnp.int32, (GMLP_W, GMLP_W), 0) // GROUP_DIM
    col_g = lax.broadcasted_iota(jnp.int32, (GMLP_W, GMLP_W), 1) // GROUP_DIM
    avg = jnp.where(row_g == col_g, 1.0 / GROUP_DIM, 0.0).astype(jnp.bfloat16)
    zg = jax.nn.gelu(z)
    cen = zg - _split_dot(zg, avg)
    var = _split_dot(cen * cen, avg)
    zn = (cen * lax.rsqrt(var + EPS) * lng_ref[...] + lnb_ref[...]).astype(jnp.bfloat16)

    t_idx = lax.broadcasted_iota(jnp.int32, (CHUNK, N_GROUPS * CHUNK), 0)
    s_idx = lax.broadcasted_iota(jnp.int32, (CHUNK, N_GROUPS * CHUNK), 1) % CHUNK
    ws = jnp.where(s_idx <= t_idx, ws_ref[...], 0.0).astype(jnp.bfloat16)
    lane_g = lax.broadcasted_iota(jnp.int32, (CHUNK, GMLP_W), 1) // GROUP_DIM
    ug = jax.nn.gelu(u)
    for c in range(ROW_TILE // CHUNK):
        rows = slice(c * CHUNK, (c + 1) * CHUNK)
        zc = zn[rows]
        zbd = jnp.concatenate(
            [jnp.where(lane_g == g, zc, jnp.zeros_like(zc)) for g in range(N_GROUPS)], axis=0)
        mixed = jnp.dot(ws, zbd, preferred_element_type=jnp.float32) + bs_ref[...]
        gm = ug[rows] * mixed
        gm_ref[rows, :] = _rms_norm(gm, gg_ref[...]).astype(jnp.bfloat16)


def _in_proj(x2, g1, w_in, lng, lnb, ws_cat, bs_exp, gg):
    T = x2.shape[0]
    const = lambda shape: pl.BlockSpec(shape, lambda i: (0, 0), pipeline_mode=pl.Buffered(1))
    row = lambda width: pl.BlockSpec((ROW_TILE, width), lambda i: (i, 0))
    return pl.pallas_call(
        _in_proj_kernel,
        grid=(T // ROW_TILE,),
        in_specs=[
            row(D_MODEL),
            const((1, D_MODEL)),
            const((D_MODEL, IN_W)),
            const((1, GMLP_W)),
            const((1, GMLP_W)),
            const((CHUNK, N_GROUPS * CHUNK)),
            const((CHUNK, GMLP_W)),
            const((1, GMLP_W)),
        ],
        out_specs=[row(ATTN_W), row(ATTN_W), row(ATTN_W), row(GMLP_W)],
        out_shape=[
            jax.ShapeDtypeStruct((T, ATTN_W), jnp.bfloat16),
            jax.ShapeDtypeStruct((T, ATTN_W), jnp.bfloat16),
            jax.ShapeDtypeStruct((T, ATTN_W), jnp.bfloat16),
            jax.ShapeDtypeStruct((T, GMLP_W), jnp.bfloat16),
        ],
        compiler_params=pltpu.CompilerParams(
            dimension_semantics=("parallel",), vmem_limit_bytes=VMEM_LIMIT_BYTES),
        name="in_proj",
    )(x2, g1, w_in, lng, lnb, ws_cat, bs_exp, gg)


def _attn_bias(window, dilation):
    assert window // dilation <= CHUNK, "band must fit in [previous, current] block"
    qi = np.arange(CHUNK)[:, None]
    kj = np.arange(2 * CHUNK)[None, :]
    steps = qi + CHUNK - kj
    band = (steps >= 0) & (steps <= window // dilation)
    alibi = (-_alibi_slopes(N_HEADS)[:, None, None]
             * (steps * dilation).astype(np.float32)[None])
    normal = np.where(band[None], alibi, -np.inf).astype(np.float32)
    first = np.where((kj >= CHUNK)[None], normal, -np.inf).astype(np.float32)
    return np.stack([normal, first], axis=0)


def _attn_kernel(q_ref, kp_ref, kc_ref, vp_ref, vc_ref, bias_ref, o_ref, lse_ref,
                 k_scr, v_scr):
    n = pl.program_id(2)
    k_scr[0:CHUNK, :] = kp_ref[...]
    k_scr[CHUNK:, :] = kc_ref[...]
    v_scr[0:CHUNK, :] = vp_ref[...]
    v_scr[CHUNK:, :] = vc_ref[...]

    lane = lax.broadcasted_iota(jnp.int32, (CHUNK, LANES), 1)
    low = lane < HEAD_DIM

    def block(i, carry):
        r0 = pl.multiple_of(i * CHUNK, CHUNK)
        first = jnp.where(jnp.logical_and(n == 0, i == 0), 1, 0)
        lse_tile = jnp.zeros((CHUNK, LANES), jnp.float32)
        for pair in range(N_PAIRS):
            cols = slice(pair * LANES, (pair + 1) * LANES)
            qp = q_ref[pl.ds(r0, CHUNK), cols]
            kp = k_scr[pl.ds(r0, 2 * CHUNK), cols]
            vp = v_scr[pl.ds(r0, 2 * CHUNK), cols]
            zero = jnp.zeros_like(qp)
            halves = []
            for half in range(2):
                h = 2 * pair + half
                qh = jnp.where(low, qp, zero) if half == 0 else jnp.where(low, zero, qp)
                s = lax.dot_general(qh, kp, (((1,), (1,)), ((), ())),
                                    preferred_element_type=jnp.float32)
                s = s + bias_ref[first, h]
                m = jnp.max(s, axis=-1, keepdims=True)
                p = jnp.exp(s - m)
                l = jnp.sum(p, axis=-1, keepdims=True)
                o = jnp.dot(p.astype(jnp.bfloat16), vp, preferred_element_type=jnp.float32)
                halves.append(o / l)
                lse_tile = jnp.where(lane == h, m + jnp.log(l), lse_tile)
            o_ref[pl.ds(r0, CHUNK), cols] = jnp.where(low, halves[0], halves[1])
        lse_ref[pl.ds(r0, CHUNK), :] = lse_tile
        return carry

    lax.fori_loop(0, Q_TILE // CHUNK, block, 0)


def _dilated_attn(q, k, v, window, dilation):
    B, S, _ = q.shape
    L = S // dilation
    assert S % dilation == 0 and L % Q_TILE == 0
    view = lambda t: t.reshape(B, L, dilation * ATTN_W)
    bias = jnp.asarray(_attn_bias(window, dilation))
    blocks_per_tile = Q_TILE // CHUNK
    cur = pl.BlockSpec((None, Q_TILE, ATTN_W), lambda b, r, n: (b, n, r))
    prev = pl.BlockSpec((None, CHUNK, ATTN_W),
                        lambda b, r, n: (b, jnp.maximum(n * blocks_per_tile - 1, 0), r))
    o, lse = pl.pallas_call(
        _attn_kernel,
        grid=(B, dilation, L // Q_TILE),
        in_specs=[
            cur, prev, cur, prev, cur,
            pl.BlockSpec((2, N_HEADS, CHUNK, 2 * CHUNK), lambda b, r, n: (0, 0, 0, 0),
                         pipeline_mode=pl.Buffered(1)),
        ],
        out_specs=[
            pl.BlockSpec((None, Q_TILE, ATTN_W), lambda b, r, n: (b, n, r)),
            pl.BlockSpec((None, Q_TILE, LANES), lambda b, r, n: (b, n, r)),
        ],
        out_shape=[
            jax.ShapeDtypeStruct((B, L, dilation * ATTN_W), jnp.float32),
            jax.ShapeDtypeStruct((B, L, dilation * LANES), jnp.float32),
        ],
        scratch_shapes=[
            pltpu.VMEM((Q_TILE + CHUNK, ATTN_W), jnp.bfloat16),
            pltpu.VMEM((Q_TILE + CHUNK, ATTN_W), jnp.bfloat16),
        ],
        compiler_params=pltpu.CompilerParams(
            dimension_semantics=("parallel", "parallel", "arbitrary"),
            vmem_limit_bytes=VMEM_LIMIT_BYTES),
        name=f"dilated_attn_d{dilation}",
    )(view(q), view(k), view(k), view(v), view(v), bias)
    return o.reshape(B, S, ATTN_W), lse.reshape(B, S, LANES)


def _post_kernel(x_ref, o1_ref, o2_ref, o3_ref, l1_ref, l2_ref, l3_ref, gm_ref,
                 ga_ref, wo_ref, g2_ref, w1_ref, w2_ref, gf_ref, out_ref):
    l1, l2, l3 = l1_ref[...], l2_ref[...], l3_ref[...]
    m = jnp.maximum(jnp.maximum(l1, l2), l3)
    e1, e2, e3 = jnp.exp(l1 - m), jnp.exp(l2 - m), jnp.exp(l3 - m)
    inv = 1.0 / (e1 + e2 + e3)
    src = lax.broadcasted_iota(jnp.int32, (LANES, ATTN_W), 0)
    dst = lax.broadcasted_iota(jnp.int32, (LANES, ATTN_W), 1) // HEAD_DIM
    expand = jnp.where(src == dst, 1.0, 0.0).astype(jnp.bfloat16)
    attn = (_split_dot(e1 * inv, expand) * o1_ref[...]
            + _split_dot(e2 * inv, expand) * o2_ref[...]
            + _split_dot(e3 * inv, expand) * o3_ref[...])

    an = _rms_norm(attn, ga_ref[...]).astype(jnp.bfloat16)
    h = (x_ref[...]
         + jnp.dot(an, wo_ref[0:ATTN_W, :], preferred_element_type=jnp.float32)
         + jnp.dot(gm_ref[...], wo_ref[ATTN_W:, :], preferred_element_type=jnp.float32))

    hn = _rms_norm(h, g2_ref[...]).astype(jnp.bfloat16)
    out_ref[...] = h
    for c in range(D_FF // FF_TILE):
        cols = slice(c * FF_TILE, (c + 1) * FF_TILE)
        f = jnp.dot(hn, w1_ref[:, cols], preferred_element_type=jnp.float32)
        f = jnp.square(jnp.maximum(f, 0.0)).astype(jnp.bfloat16)
        out_ref[...] += jnp.dot(f, w2_ref[cols, :], preferred_element_type=jnp.float32)
    out_ref[...] = _rms_norm(out_ref[...], gf_ref[...])


def _post(x2, o1, o2, o3, l1, l2, l3, gm, ga, wo, g2, w1, w2, gf):
    T = x2.shape[0]
    const = lambda shape: pl.BlockSpec(shape, lambda i: (0, 0), pipeline_mode=pl.Buffered(1))
    row = lambda width: pl.BlockSpec((ROW_TILE, width), lambda i: (i, 0))
    return pl.pallas_call(
        _post_kernel,
        grid=(T // ROW_TILE,),
        in_specs=[
            row(D_MODEL), row(ATTN_W), row(ATTN_W), row(ATTN_W),
            row(LANES), row(LANES), row(LANES), row(GMLP_W),
            const((1, ATTN_W)),
            const((D_MODEL, D_MODEL)),
            const((1, D_MODEL)),
            const((D_MODEL, D_FF)),
            const((D_FF, D_MODEL)),
            const((1, D_MODEL)),
        ],
        out_specs=row(D_MODEL),
        out_shape=jax.ShapeDtypeStruct((T, D_MODEL), jnp.float32),
        compiler_params=pltpu.CompilerParams(
            dimension_semantics=("parallel",), vmem_limit_bytes=VMEM_LIMIT_BYTES),
        name="post",
    )(x2, o1, o2, o3, l1, l2, l3, gm, ga, wo, g2, w1, w2, gf)


def kernel(x, norm1_g, w_in, sgu_ln_g, sgu_ln_b, sgu_w, sgu_b, attn_out_g, gmlp_out_g,
           w_out, norm2_g, w_ff1, w_ff2, final_norm_g):
    B, S, _ = x.shape
    T = B * S
    assert norm1_g.shape[0] == 1, "single-layer block"
    bf16 = jnp.bfloat16
    x2 = x.reshape(T, D_MODEL)

    ws_cat = jnp.transpose(sgu_w[0], (1, 0, 2)).reshape(CHUNK, N_GROUPS * CHUNK)
    bs_exp = jnp.repeat(sgu_b[0].T, GROUP_DIM, axis=1)

    q, k, v, gm = _in_proj(
        x2, norm1_g[0].reshape(1, D_MODEL), w_in[0].astype(bf16),
        sgu_ln_g[0].reshape(1, GMLP_W), sgu_ln_b[0].reshape(1, GMLP_W),
        ws_cat, bs_exp, gmlp_out_g[0].reshape(1, GMLP_W))

    q3, k3, v3 = (t.reshape(B, S, ATTN_W) for t in (q, k, v))
    outs, lses = [], []
    for window, dilation in PATTERNS:
        o, lse = _dilated_attn(q3, k3, v3, window, dilation)
        outs.append(o.reshape(T, ATTN_W))
        lses.append(lse.reshape(T, LANES))

    out = _post(
        x2, *outs, *lses, gm,
        attn_out_g[0].reshape(1, ATTN_W), w_out[0].astype(bf16),
        norm2_g[0].reshape(1, D_MODEL), w_ff1[0].astype(bf16), w_ff2[0].astype(bf16),
        final_norm_g.reshape(1, D_MODEL))
    return out.reshape(B, S, D_MODEL)
```

```python
import math

import jax
import jax.numpy as jnp
import numpy as np
from jax import lax
from jax.experimental import pallas as pl
from jax.experimental.pallas import tpu as pltpu

D_MODEL = 1024
HEAD_DIM = 64
N_HEADS = 12
N_GROUPS = 4
GROUP_DIM = 64
ATTN_W = N_HEADS * HEAD_DIM
GMLP_W = N_GROUPS * GROUP_DIM
IN_W = 3 * ATTN_W + 2 * GMLP_W
CHUNK = 128
PATTERNS = ((128, 1), (512, 4), (2048, 16))
D_FF = 4 * D_MODEL
EPS = 1e-6

LANES = 128
VMEM_LIMIT_BYTES = 60000 * 1024

ROW_TILE = 512
Q_TILE = ROW_TILE
FF_TILE = 1024
N_PAIRS = N_HEADS // 2
N_SLABS = ATTN_W // LANES


def _alibi_slopes(n):
    def pow2_slopes(m):
        start = 2.0 ** (-8.0 / m)
        return [start ** (i + 1) for i in range(m)]
    if math.log2(n).is_integer():
        s = pow2_slopes(n)
    else:
        c = 2 ** int(math.floor(math.log2(n)))
        s = pow2_slopes(c) + pow2_slopes(2 * c)[0::2][: n - c]
    return np.asarray(s, dtype=np.float32)


def _rms_norm(x, g):
    ms = jnp.mean(x * x, axis=-1, keepdims=True)
    return x * lax.rsqrt(ms + EPS) * g


def _split_dot(x, m):
    hi = x.astype(jnp.bfloat16)
    lo = (x - hi.astype(jnp.float32)).astype(jnp.bfloat16)
    return (jnp.dot(hi, m, preferred_element_type=jnp.float32)
            + jnp.dot(lo, m, preferred_element_type=jnp.float32))


def _split_shape(d, width):
    return (d, ROW_TILE // d, width)


def _in_proj_kernel(x_ref, g1_ref, w_ref, lng_ref, lnb_ref, ws_ref, bs_ref, gg_ref,
                    *refs):
    n_pat = len(PATTERNS)
    qkv_refs = [refs[i * n_pat:(i + 1) * n_pat] for i in range(3)]
    gm_ref, slab_ref = refs[3 * n_pat], refs[3 * n_pat + 1]
    x = x_ref[...]
    hn = _rms_norm(x, g1_ref[...]).astype(jnp.bfloat16)

    def proj(lo, width):
        return jnp.dot(hn, w_ref[:, lo:lo + width], preferred_element_type=jnp.float32)

    for which, outs in enumerate(qkv_refs):
        res = proj(which * ATTN_W, ATTN_W)
        if which == 0:
            res = res * (HEAD_DIM ** -0.5)
        for s in range(N_SLABS):
            slab_ref[s] = res[:, s * LANES:(s + 1) * LANES]
        for (_, d), out in zip(PATTERNS, outs):
            if d == 1:
                out[0] = res.astype(jnp.bfloat16)
                continue
            for s in range(N_SLABS):
                for r in range(d):
                    rows = slab_ref[s, pl.ds(r, ROW_TILE // d, stride=d), :]
                    out[r, :, s * LANES:(s + 1) * LANES] = rows.astype(jnp.bfloat16)

    u = proj(3 * ATTN_W, GMLP_W)
    z = proj(3 * ATTN_W + GMLP_W, GMLP_W)

    row_g = lax.broadcasted_iota(jnp.int32, (GMLP_W, GMLP_W), 0) // GROUP_DIM
    col_g = lax.broadcasted_iota(jnp.int32, (GMLP_W, GMLP_W), 1) // GROUP_DIM
    avg = jnp.where(row_g == col_g, 1.0 / GROUP_DIM, 0.0).astype(jnp.bfloat16)
    zg = jax.nn.gelu(z)
    cen = zg - _split_dot(zg, avg)
    var = _split_dot(cen * cen, avg)
    zn = (cen * lax.rsqrt(var + EPS) * lng_ref[...] + lnb_ref[...]).astype(jnp.bfloat16)

    t_idx = lax.broadcasted_iota(jnp.int32, (CHUNK, N_GROUPS * CHUNK), 0)
    s_idx = lax.broadcasted_iota(jnp.int32, (CHUNK, N_GROUPS * CHUNK), 1) % CHUNK
    ws = jnp.where(s_idx <= t_idx, ws_ref[...], 0.0).astype(jnp.bfloat16)
    lane_g = lax.broadcasted_iota(jnp.int32, (CHUNK, GMLP_W), 1) // GROUP_DIM
    ug = jax.nn.gelu(u)
    for c in range(ROW_TILE // CHUNK):
        rows = slice(c * CHUNK, (c + 1) * CHUNK)
        zc = zn[rows]
        zbd = jnp.concatenate(
            [jnp.where(lane_g == g, zc, jnp.zeros_like(zc)) for g in range(N_GROUPS)], axis=0)
        mixed = jnp.dot(ws, zbd, preferred_element_type=jnp.float32) + bs_ref[...]
        gm = ug[rows] * mixed
        gm_ref[rows, :] = _rms_norm(gm, gg_ref[...]).astype(jnp.bfloat16)


def _in_proj(x2, g1, w_in, lng, lnb, ws_cat, bs_exp, gg):
    T = x2.shape[0]
    n_tiles = T // ROW_TILE
    const = lambda shape: pl.BlockSpec(shape, lambda i: (0, 0), pipeline_mode=pl.Buffered(1))
    row = lambda width: pl.BlockSpec((ROW_TILE, width), lambda i: (i, 0))
    split_specs, split_shapes = [], []
    for _ in range(3):
        for _, d in PATTERNS:
            shape = _split_shape(d, ATTN_W)
            split_specs.append(pl.BlockSpec((None,) + shape, lambda i: (i, 0, 0, 0)))
            split_shapes.append(jax.ShapeDtypeStruct((n_tiles,) + shape, jnp.bfloat16))
    outs = pl.pallas_call(
        _in_proj_kernel,
        grid=(n_tiles,),
        in_specs=[
            row(D_MODEL),
            const((1, D_MODEL)),
            const((D_MODEL, IN_W)),
            const((1, GMLP_W)),
            const((1, GMLP_W)),
            const((CHUNK, N_GROUPS * CHUNK)),
            const((CHUNK, GMLP_W)),
            const((1, GMLP_W)),
        ],
        out_specs=split_specs + [row(GMLP_W)],
        out_shape=split_shapes + [jax.ShapeDtypeStruct((T, GMLP_W), jnp.bfloat16)],
        scratch_shapes=[pltpu.VMEM((N_SLABS, ROW_TILE, LANES), jnp.float32)],
        compiler_params=pltpu.CompilerParams(
            dimension_semantics=("parallel",), vmem_limit_bytes=VMEM_LIMIT_BYTES),
        name="in_proj",
    )(x2, g1, w_in, lng, lnb, ws_cat, bs_exp, gg)
    n_pat = len(PATTERNS)
    q, k, v = (outs[i * n_pat:(i + 1) * n_pat] for i in range(3))
    return q, k, v, outs[-1]


def _attn_bias(window, dilation):
    assert window // dilation <= CHUNK, "band must fit in [previous, current] block"
    qi = np.arange(CHUNK)[:, None]
    kj = np.arange(2 * CHUNK)[None, :]
    steps = qi + CHUNK - kj
    band = (steps >= 0) & (steps <= window // dilation)
    alibi = (-_alibi_slopes(N_HEADS)[:, None, None]
             * (steps * dilation).astype(np.float32)[None])
    normal = np.where(band[None], alibi, -np.inf).astype(np.float32)
    first = np.where((kj >= CHUNK)[None], normal, -np.inf).astype(np.float32)
    return np.stack([normal, first], axis=0)


def _tile_rows(i, rows_per_piece):
    if rows_per_piece >= CHUNK:
        per_piece = rows_per_piece // CHUNK
        return i // per_piece, pl.ds(pl.multiple_of((i % per_piece) * CHUNK, CHUNK), CHUNK)
    pieces = CHUNK // rows_per_piece
    return pl.ds(i * pieces, pieces), slice(None)


def _load_block(ref, i, cols):
    piece, rows = _tile_rows(i, ref.shape[1])
    return ref[piece, rows, cols].reshape(CHUNK, -1)


def _store_block(ref, i, cols, val):
    piece, rows = _tile_rows(i, ref.shape[1])
    if ref.shape[1] < CHUNK:
        val = val.reshape(CHUNK // ref.shape[1], ref.shape[1], -1)
    ref[piece, rows, cols] = val


def _copy_pieces(dst, row0, src):
    if len(src.shape) == 2:
        dst[row0:row0 + src.shape[0], :] = src[...]
        return
    rows = src.shape[1]
    for t in range(src.shape[0]):
        dst[row0 + t * rows:row0 + (t + 1) * rows, :] = src[t]


def _attn_kernel(q_ref, kp_ref, kc_ref, vp_ref, vc_ref, bias_ref, o_ref, lse_ref,
                 k_scr, v_scr):
    n = pl.program_id(2)
    _copy_pieces(k_scr, 0, kp_ref)
    _copy_pieces(k_scr, CHUNK, kc_ref)
    _copy_pieces(v_scr, 0, vp_ref)
    _copy_pieces(v_scr, CHUNK, vc_ref)

    lane = lax.broadcasted_iota(jnp.int32, (CHUNK, LANES), 1)
    low = lane < HEAD_DIM

    def block(i, carry):
        r0 = pl.multiple_of(i * CHUNK, CHUNK)
        first = jnp.where(jnp.logical_and(n == 0, i == 0), 1, 0)
        lse_tile = jnp.zeros((CHUNK, LANES), jnp.float32)
        for pair in range(N_PAIRS):
            cols = slice(pair * LANES, (pair + 1) * LANES)
            qp = _load_block(q_ref, i, cols)
            kp = k_scr[pl.ds(r0, 2 * CHUNK), cols]
            vp = v_scr[pl.ds(r0, 2 * CHUNK), cols]
            zero = jnp.zeros_like(qp)
            halves = []
            for half in range(2):
                h = 2 * pair + half
                qh = jnp.where(low, qp, zero) if half == 0 else jnp.where(low, zero, qp)
                s = lax.dot_general(qh, kp, (((1,), (1,)), ((), ())),
                                    preferred_element_type=jnp.float32)
                s = s + bias_ref[first, h]
                m = jnp.max(s, axis=-1, keepdims=True)
                p = jnp.exp(s - m)
                l = jnp.sum(p, axis=-1, keepdims=True)
                o = jnp.dot(p.astype(jnp.bfloat16), vp, preferred_element_type=jnp.float32)
                halves.append(o / l)
                lse_tile = jnp.where(lane == h, m + jnp.log(l), lse_tile)
            _store_block(o_ref, i, cols, jnp.where(low, halves[0], halves[1]))
        _store_block(lse_ref, i, slice(None), lse_tile)
        return carry

    lax.fori_loop(0, Q_TILE // CHUNK, block, 0)


def _dilated_attn(q, k, v, B, window, dilation):
    d = dilation
    n_tiles = q.shape[0] // B
    R = ROW_TILE // d
    assert n_tiles % d == 0, "sequence must hold whole Q_TILE blocks of every residue"
    five = lambda t: t.reshape((B, n_tiles) + t.shape[1:])
    bias = jnp.asarray(_attn_bias(window, d))
    cur = lambda width: pl.BlockSpec((None, d, None, R, width), lambda b, r, n: (b, n, r, 0, 0))
    if R >= CHUNK:
        prev = pl.BlockSpec(
            (None, None, None, CHUNK, ATTN_W),
            lambda b, r, n: (b, jnp.maximum(n * d - 1, 0), r, R // CHUNK - 1, 0))
    else:
        per_block = CHUNK // R
        prev = pl.BlockSpec(
            (None, per_block, None, R, ATTN_W),
            lambda b, r, n: (b, jnp.maximum(n * (d // per_block) - 1, 0), r, 0, 0))
    o, lse = pl.pallas_call(
        _attn_kernel,
        grid=(B, d, n_tiles // d),
        in_specs=[
            cur(ATTN_W), prev, cur(ATTN_W), prev, cur(ATTN_W),
            pl.BlockSpec((2, N_HEADS, CHUNK, 2 * CHUNK), lambda b, r, n: (0, 0, 0, 0),
                         pipeline_mode=pl.Buffered(1)),
        ],
        out_specs=[cur(ATTN_W), cur(LANES)],
        out_shape=[
            jax.ShapeDtypeStruct((B, n_tiles, d, R, ATTN_W), jnp.float32),
            jax.ShapeDtypeStruct((B, n_tiles, d, R, LANES), jnp.float32),
        ],
        scratch_shapes=[
            pltpu.VMEM((Q_TILE + CHUNK, ATTN_W), jnp.bfloat16),
            pltpu.VMEM((Q_TILE + CHUNK, ATTN_W), jnp.bfloat16),
        ],
        compiler_params=pltpu.CompilerParams(
            dimension_semantics=("parallel", "parallel", "arbitrary"),
            vmem_limit_bytes=VMEM_LIMIT_BYTES),
        name=f"dilated_attn_d{d}",
    )(five(q), five(k), five(k), five(v), five(v), bias)
    four = lambda t: t.reshape((B * n_tiles,) + t.shape[2:])
    return four(o), four(lse)


def _post_kernel(x_ref, *refs):
    n_pat = len(PATTERNS)
    o_refs, l_refs = refs[:n_pat], refs[n_pat:2 * n_pat]
    (gm_ref, ga_ref, wo_ref, g2_ref, w1_ref, w2_ref, gf_ref,
     out_ref, o_nat, l_nat) = refs[2 * n_pat:]

    for i, (_, d) in enumerate(PATTERNS):
        for r in range(d):
            rows = pl.ds(r, ROW_TILE // d, stride=d) if d > 1 else slice(None)
            l_nat[i, rows, :] = l_refs[i][r]
            for s in range(N_SLABS):
                o_nat[i, s, rows, :] = o_refs[i][r, :, s * LANES:(s + 1) * LANES]

    ls = [l_nat[i] for i in range(n_pat)]
    m = ls[0]
    for l in ls[1:]:
        m = jnp.maximum(m, l)
    es = [jnp.exp(l - m) for l in ls]
    inv = 1.0 / sum(es[1:], es[0])
    src = lax.broadcasted_iota(jnp.int32, (LANES, ATTN_W), 0)
    dst = lax.broadcasted_iota(jnp.int32, (LANES, ATTN_W), 1) // HEAD_DIM
    expand = jnp.where(src == dst, 1.0, 0.0).astype(jnp.bfloat16)
    attn = None
    for i in range(n_pat):
        o_i = jnp.concatenate([o_nat[i, s] for s in range(N_SLABS)], axis=1)
        term = _split_dot(es[i] * inv, expand) * o_i
        attn = term if attn is None else attn + term

    an = _rms_norm(attn, ga_ref[...]).astype(jnp.bfloat16)
    h = (x_ref[...]
         + jnp.dot(an, wo_ref[0:ATTN_W, :], preferred_element_type=jnp.float32)
         + jnp.dot(gm_ref[...], wo_ref[ATTN_W:, :], preferred_element_type=jnp.float32))

    hn = _rms_norm(h, g2_ref[...]).astype(jnp.bfloat16)
    out_ref[...] = h
    for c in range(D_FF // FF_TILE):
        cols = slice(c * FF_TILE, (c + 1) * FF_TILE)
        f = jnp.dot(hn, w1_ref[:, cols], preferred_element_type=jnp.float32)
        f = jnp.square(jnp.maximum(f, 0.0)).astype(jnp.bfloat16)
        out_ref[...] += jnp.dot(f, w2_ref[cols, :], preferred_element_type=jnp.float32)
    out_ref[...] = _rms_norm(out_ref[...], gf_ref[...])


def _post(x2, outs, lses, gm, ga, wo, g2, w1, w2, gf):
    T = x2.shape[0]
    n_pat = len(PATTERNS)
    const = lambda shape: pl.BlockSpec(shape, lambda i: (0, 0), pipeline_mode=pl.Buffered(1))
    row = lambda width: pl.BlockSpec((ROW_TILE, width), lambda i: (i, 0))
    split = lambda d, width: pl.BlockSpec((None,) + _split_shape(d, width),
                                          lambda i: (i, 0, 0, 0))
    return pl.pallas_call(
        _post_kernel,
        grid=(T // ROW_TILE,),
        in_specs=(
            [row(D_MODEL)]
            + [split(d, ATTN_W) for _, d in PATTERNS]
            + [split(d, LANES) for _, d in PATTERNS]
            + [row(GMLP_W),
               const((1, ATTN_W)),
               const((D_MODEL, D_MODEL)),
               const((1, D_MODEL)),
               const((D_MODEL, D_FF)),
               const((D_FF, D_MODEL)),
               const((1, D_MODEL))]),
        out_specs=row(D_MODEL),
        out_shape=jax.ShapeDtypeStruct((T, D_MODEL), jnp.float32),
        scratch_shapes=[
            pltpu.VMEM((n_pat, N_SLABS, ROW_TILE, LANES), jnp.float32),
            pltpu.VMEM((n_pat, ROW_TILE, LANES), jnp.float32),
        ],
        compiler_params=pltpu.CompilerParams(
            dimension_semantics=("parallel",), vmem_limit_bytes=VMEM_LIMIT_BYTES),
        name="post",
    )(x2, *outs, *lses, gm, ga, wo, g2, w1, w2, gf)


def kernel(x, norm1_g, w_in, sgu_ln_g, sgu_ln_b, sgu_w, sgu_b, attn_out_g, gmlp_out_g,
           w_out, norm2_g, w_ff1, w_ff2, final_norm_g):
    B, S, _ = x.shape
    T = B * S
    assert norm1_g.shape[0] == 1, "single-layer block"
    assert S % ROW_TILE == 0
    bf16 = jnp.bfloat16
    x2 = x.reshape(T, D_MODEL)

    ws_cat = jnp.transpose(sgu_w[0], (1, 0, 2)).reshape(CHUNK, N_GROUPS * CHUNK)
    bs_exp = jnp.repeat(sgu_b[0].T, GROUP_DIM, axis=1)

    q, k, v, gm = _in_proj(
        x2, norm1_g[0].reshape(1, D_MODEL), w_in[0].astype(bf16),
        sgu_ln_g[0].reshape(1, GMLP_W), sgu_ln_b[0].reshape(1, GMLP_W),
        ws_cat, bs_exp, gmlp_out_g[0].reshape(1, GMLP_W))

    outs, lses = [], []
    for i, (window, dilation) in enumerate(PATTERNS):
        o, lse = _dilated_attn(q[i], k[i], v[i], B, window, dilation)
        outs.append(o)
        lses.append(lse)

    out = _post(
        x2, outs, lses, gm,
        attn_out_g[0].reshape(1, ATTN_W), w_out[0].astype(bf16),
        norm2_g[0].reshape(1, D_MODEL), w_ff1[0].astype(bf16), w_ff2[0].astype(bf16),
        final_norm_g.reshape(1, D_MODEL))
    return out.reshape(B, S, D_MODEL)
```
